```python
import math
import jax
import jax.numpy as jnp
from jax import lax
import numpy as np

D_MODEL = 1024
BATCH = 16
SEQ = 2048
DEPTH = 4
DEC_BATCH = 128
DEC_SEQ = 1
PAST_LEN = 8192
PAGE_SIZE = 128

N_MIXERS = 4
EXPAND = 2
D_BRANCH = EXPAND * D_MODEL
RMS_EPS = 1e-6
LN_EPS = 1e-5
CONV_W = 31
RWKV_HEAD = 64
RWKV_HEADS = D_BRANCH // RWKV_HEAD
DECAY_LORA = 64
ICLR_LORA = 64
GN_EPS = 64e-5
MLA_HEADS = 16
QK_NOPE = 128
QK_ROPE = 64
V_HEAD = D_BRANCH // MLA_HEADS
KV_LORA = D_MODEL // 4
Q_LORA = 3 * D_MODEL // 8
ROPE_THETA = 10000.0
MLA_SCALE = (QK_NOPE + QK_ROPE) ** -0.5
Q_BLOCK = 128
NEG_INF = -1e30
GROUP_CH = 16
N_GROUPS = D_BRANCH // GROUP_CH
SSM_STATE = 64
DT_MIN = 0.001
DT_MAX = 0.1

kernel_name = 'hybrid_conv_rwkv7_mla_s5_step'


def rmsnorm(x, g):
    xf = x.astype(jnp.float32)
    y = xf * lax.rsqrt(jnp.mean(xf * xf, axis=-1, keepdims=True) + RMS_EPS)
    return (y * g.astype(jnp.float32)).astype(x.dtype)


def layernorm(x, g, b):
    xf = x.astype(jnp.float32)
    mu = jnp.mean(xf, axis=-1, keepdims=True)
    var = jnp.mean(jnp.square(xf - mu), axis=-1, keepdims=True)
    y = (xf - mu) * lax.rsqrt(var + LN_EPS)
    return (y * g.astype(jnp.float32) + b.astype(jnp.float32)).astype(x.dtype)


def rope(x, pos):
    half = QK_ROPE // 2
    inv_freq = ROPE_THETA ** (-jnp.arange(half, dtype=jnp.float32) / half)
    ang = pos.astype(jnp.float32)[:, None] * inv_freq[None, :]
    cos = jnp.cos(ang)[:, None, :]
    sin = jnp.sin(ang)[:, None, :]
    x1 = x[..., :half].astype(jnp.float32)
    x2 = x[..., half:].astype(jnp.float32)
    return jnp.concatenate([x1 * cos - x2 * sin, x1 * sin + x2 * cos], axis=-1).astype(x.dtype)


def conv_branch(h, buf, w_in, b_in, conv_w, conv_b, ln_g, ln_b, w_out):
    ga, gb, z = jnp.split(h @ w_in + b_in, 3, axis=-1)
    u = ga * jax.nn.sigmoid(gb)
    if buf is None:
        buf = jnp.zeros((u.shape[0], CONV_W - 1, u.shape[-1]), u.dtype)
    ext = jnp.concatenate([buf.astype(u.dtype), u], axis=1)
    c = lax.conv_general_dilated(ext, conv_w[:, None, :], window_strides=(1,), padding='VALID',
                                 dimension_numbers=('NWC', 'WIO', 'NWC'),
                                 feature_group_count=u.shape[-1]) + conv_b
    c = jax.nn.silu(layernorm(c, ln_g, ln_b))
    out = (c * jax.nn.silu(z)) @ w_out
    return out, ext[:, -(CONV_W - 1):]


def rwkv7_branch(h, shift, wkv, mu, w_rkvz, w0, w1, w2, a0, a1, a2, k_k, k_a, r_k, ln_g, ln_b, w_out):
    bsz, t, _ = h.shape
    if shift is None:
        shift = jnp.zeros((bsz, h.shape[-1]), h.dtype)
    if wkv is None:
        wkv = jnp.zeros((bsz, RWKV_HEADS, RWKV_HEAD, RWKV_HEAD), jnp.float32)
    h_prev = jnp.concatenate([shift[:, None, :].astype(h.dtype), h[:, :-1]], axis=1)
    xs = h[:, :, None, :] + (h_prev - h)[:, :, None, :] * mu
    rkvz = jnp.einsum('btjd,jde->btje', xs[:, :, :4], w_rkvz)
    r, k, v, z = rkvz[:, :, 0], rkvz[:, :, 1], rkvz[:, :, 2], rkvz[:, :, 3]
    xw, xa = xs[:, :, 4], xs[:, :, 5]
    w_log = -jax.nn.softplus(-(w0 + jnp.tanh(xw @ w1) @ w2)) - 0.5
    decay = jnp.exp(-jnp.exp(w_log.astype(jnp.float32)))
    a = jax.nn.sigmoid(a0 + (xa @ a1) @ a2)

    def heads(y):
        return y.reshape(bsz, t, RWKV_HEADS, RWKV_HEAD)

    kkf = heads(k * k_k).astype(jnp.float32)
    kk = kkf / jnp.maximum(jnp.sqrt(jnp.sum(kkf * kkf, axis=-1, keepdims=True)), 1e-12)
    k = k * (1.0 + (a - 1.0) * k_a)
    rh = heads(r).astype(jnp.float32)
    kh = heads(k).astype(jnp.float32)
    vh = heads(v).astype(jnp.float32)
    bv = kk * heads(a).astype(jnp.float32)

    def step(S, inp):
        r_t, w_t, k_t, v_t, kk_t, b_t = inp
        sa = jnp.einsum('bhij,bhj->bhi', S, -kk_t)
        S = S * w_t[:, :, None, :] + sa[..., None] * b_t[:, :, None, :] + v_t[..., None] * k_t[:, :, None, :]
        return S, jnp.einsum('bhij,bhj->bhi', S, r_t)

    seq = tuple(jnp.moveaxis(y, 1, 0) for y in (rh, heads(decay), kh, vh, kk, bv))
    s_fin, y = lax.scan(step, wkv.astype(jnp.float32), seq)
    y = jnp.moveaxis(y, 0, 1)
    mu_y = jnp.mean(y, axis=-1, keepdims=True)
    var_y = jnp.mean(jnp.square(y - mu_y), axis=-1, keepdims=True)
    y = ((y - mu_y) * lax.rsqrt(var_y + GN_EPS)).reshape(bsz, t, D_BRANCH)
    y = y * ln_g.astype(jnp.float32) + ln_b.astype(jnp.float32)
    bonus = jnp.sum(rh * kh * r_k.astype(jnp.float32), axis=-1, keepdims=True) * vh
    y = (y + bonus.reshape(bsz, t, D_BRANCH)).astype(h.dtype)
    out = (y * jax.nn.silu(z)) @ w_out
    return out, h[:, -1], s_fin


def mla_attend(q_lat, q_rope, ckv, kr, q_pos, k_pos):
    s = jnp.einsum('bthc,blc->bhtl', q_lat, ckv, preferred_element_type=jnp.float32)
    s = s + jnp.einsum('bthr,blr->bhtl', q_rope, kr, preferred_element_type=jnp.float32)
    s = jnp.where(k_pos[None, :] <= q_pos[:, None], s * MLA_SCALE, NEG_INF)
    p = jax.nn.softmax(s, axis=-1)
    return jnp.einsum('bhtl,blc->bthc', p.astype(ckv.dtype), ckv)


def mla_branch(h, pos, past_ckv, past_kr, w_in, q_norm, kv_norm, w_uq, w_uk, w_uv, w_out):
    bsz, t, _ = h.shape
    cq, ckv, kr, z = jnp.split(h @ w_in, [Q_LORA, Q_LORA + KV_LORA, Q_LORA + KV_LORA + QK_ROPE], axis=-1)
    q = jnp.einsum('btq,qhd->bthd', rmsnorm(cq, q_norm), w_uq)
    q_lat = jnp.einsum('bthn,chn->bthc', q[..., :QK_NOPE], w_uk)
    q_rope = rope(q[..., QK_NOPE:], pos)
    ckv = rmsnorm(ckv, kv_norm)
    kr = rope(kr[:, :, None, :], pos)[:, :, 0]
    if past_ckv is None:
        nblk = t // Q_BLOCK

        def to_blocks(y):
            return jnp.moveaxis(y.reshape((bsz, nblk, Q_BLOCK) + y.shape[2:]), 1, 0)

        o_lat = lax.map(lambda blk: mla_attend(blk[0], blk[1], ckv, kr, blk[2], pos),
                        (to_blocks(q_lat), to_blocks(q_rope), pos.reshape(nblk, Q_BLOCK)))
        o_lat = jnp.moveaxis(o_lat, 0, 1).reshape(bsz, t, MLA_HEADS, KV_LORA)
    else:
        keys_c = jnp.concatenate([past_ckv.astype(ckv.dtype), ckv], axis=1)
        keys_r = jnp.concatenate([past_kr.astype(kr.dtype), kr], axis=1)
        k_pos = jnp.arange(keys_c.shape[1], dtype=jnp.int32)
        o_lat = mla_attend(q_lat, q_rope, keys_c, keys_r, pos, k_pos)
    o = jnp.einsum('bthc,chv->bthv', o_lat, w_uv).reshape(bsz, t, D_BRANCH)
    out = (o * jax.nn.silu(z)) @ w_out
    return out, ckv, kr


def s5_branch(h, s0_re, s0_im, w_in, lam_re, lam_im, log_dt, b_re, b_im, c_re, c_im, d_skip, w_glu, b_glu, w_out):
    bsz, t, _ = h.shape
    u, z = jnp.split(h @ w_in, 2, axis=-1)
    lr = lam_re.astype(jnp.float32)
    li = lam_im.astype(jnp.float32)
    dt = jnp.exp(log_dt.astype(jnp.float32))[:, None]
    mag = jnp.exp(lr * dt)
    ab_re = mag * jnp.cos(li * dt)
    ab_im = mag * jnp.sin(li * dt)
    den = lr * lr + li * li
    nr = ab_re - 1.0
    f_re = (nr * lr + ab_im * li) / den
    f_im = (ab_im * lr - nr * li) / den
    br = b_re.astype(jnp.float32)
    bi = b_im.astype(jnp.float32)
    bb_re = f_re[..., None] * br - f_im[..., None] * bi
    bb_im = f_re[..., None] * bi + f_im[..., None] * br
    uf = u.astype(jnp.float32)
    ug = uf.reshape(bsz, t, N_GROUPS, GROUP_CH)
    bu_re = jnp.einsum('btgc,gpc->btgp', ug, bb_re)
    bu_im = jnp.einsum('btgc,gpc->btgp', ug, bb_im)
    a_re = jnp.broadcast_to(ab_re, (1, t, N_GROUPS, SSM_STATE))
    a_im = jnp.broadcast_to(ab_im, (1, t, N_GROUPS, SSM_STATE))

    def combine(e1, e2):
        a1r, a1i, b1r, b1i = e1
        a2r, a2i, b2r, b2i = e2
        return (a2r * a1r - a2i * a1i, a2r * a1i + a2i * a1r,
                a2r * b1r - a2i * b1i + b2r, a2r * b1i + a2i * b1r + b2i)

    p_re, p_im, s_re, s_im = lax.associative_scan(combine, (a_re, a_im, bu_re, bu_im), axis=1)
    if s0_re is not None:
        h0r = s0_re.astype(jnp.float32)[:, None]
        h0i = s0_im.astype(jnp.float32)[:, None]
        s_re, s_im = s_re + p_re * h0r - p_im * h0i, s_im + p_re * h0i + p_im * h0r
    y = (jnp.einsum('btgp,gcp->btgc', s_re, c_re.astype(jnp.float32))
         - jnp.einsum('btgp,gcp->btgc', s_im, c_im.astype(jnp.float32)))
    y = y.reshape(bsz, t, D_BRANCH) + d_skip.astype(jnp.float32) * uf
    g = jax.nn.gelu(y).astype(h.dtype)
    y = g * jax.nn.sigmoid(g @ w_glu + b_glu)
    out = (y * jax.nn.silu(z)) @ w_out
    return out, s_re[:, -1], s_im[:, -1]


def setup_inputs(seed: int = 0) -> dict:
    key = jax.random.key(seed)
    ks = iter(jax.random.split(key, 80))
    f32 = jnp.float32
    E = D_BRANCH
    D = D_MODEL

    def nrm(shape, scale):
        return scale * jax.random.normal(next(ks), shape, f32)

    def gain(shape):
        return 1.0 + nrm(shape, 0.01)

    n_pages = PAST_LEN // PAGE_SIZE
    n_used = DEC_BATCH * n_pages
    n_pool = n_used + max(1, n_used // 4)
    page_table = jax.random.permutation(next(ks), n_pool)[:n_used].reshape(DEC_BATCH, n_pages).astype(jnp.int32)
    return {
        'x_prompt': nrm((BATCH, SEQ, D), 1.0),
        'x_sample': nrm((DEC_BATCH, DEC_SEQ, D), 1.0),
        'state_conv': nrm((DEC_BATCH, CONV_W - 1, E), 0.5),
        'state_shift': nrm((DEC_BATCH, D), 1.0),
        'state_wkv': nrm((DEC_BATCH, RWKV_HEADS, RWKV_HEAD, RWKV_HEAD), 0.3),
        'cache_ckv': nrm((n_pool, PAGE_SIZE, KV_LORA), 1.0),
        'cache_krope': nrm((n_pool, PAGE_SIZE, QK_ROPE), 1.0),
        'state_ssm_re': nrm((DEC_BATCH, N_GROUPS, SSM_STATE), 0.5),
        'state_ssm_im': nrm((DEC_BATCH, N_GROUPS, SSM_STATE), 0.5),
        'page_table': page_table,
        'norm_pre': gain((DEPTH, D)),
        'norm_post': gain((DEPTH, D)),
        'a_w_in': nrm((D, 3 * E), D ** -0.5),
        'a_b_in': nrm((3 * E,), 0.01),
        'a_conv_w': nrm((CONV_W, E), CONV_W ** -0.5),
        'a_conv_b': nrm((E,), 0.01),
        'a_ln_g': gain((E,)),
        'a_ln_b': nrm((E,), 0.01),
        'a_w_out': nrm((E, D), E ** -0.5),
        'b_mu': jax.random.uniform(next(ks), (6, D), f32),
        'b_w_rkvz': nrm((4, D, E), D ** -0.5),
        'b_w0': -1.0 + nrm((E,), 0.5),
        'b_w1': nrm((D, DECAY_LORA), D ** -0.5),
        'b_w2': nrm((DECAY_LORA, E), 0.1 * DECAY_LORA ** -0.5),
        'b_a0': nrm((E,), 0.1),
        'b_a1': nrm((D, ICLR_LORA), D ** -0.5),
        'b_a2': nrm((ICLR_LORA, E), 0.1 * ICLR_LORA ** -0.5),
        'b_k_k': 0.85 + nrm((E,), 0.05),
        'b_k_a': 1.0 + nrm((E,), 0.05),
        'b_r_k': nrm((RWKV_HEADS, RWKV_HEAD), 0.1),
        'b_ln_g': gain((E,)),
        'b_ln_b': nrm((E,), 0.01),
        'b_w_out': nrm((E, D), E ** -0.5),
        'c_w_in': nrm((D, Q_LORA + KV_LORA + QK_ROPE + E), D ** -0.5),
        'c_q_norm': gain((Q_LORA,)),
        'c_kv_norm': gain((KV_LORA,)),
        'c_w_uq': nrm((Q_LORA, MLA_HEADS, QK_NOPE + QK_ROPE), Q_LORA ** -0.5),
        'c_w_uk': nrm((KV_LORA, MLA_HEADS, QK_NOPE), KV_LORA ** -0.5),
        'c_w_uv': nrm((KV_LORA, MLA_HEADS, V_HEAD), KV_LORA ** -0.5),
        'c_w_out': nrm((E, D), E ** -0.5),
        'd_w_in': nrm((D, 2 * E), D ** -0.5),
        'd_lambda_re': -0.5 + nrm((N_GROUPS, SSM_STATE), 0.01),
        'd_lambda_im': math.pi * jnp.arange(SSM_STATE, dtype=f32)[None, :] + nrm((N_GROUPS, SSM_STATE), 0.01),
        'd_log_dt': jax.random.uniform(next(ks), (N_GROUPS,), f32, math.log(DT_MIN), math.log(DT_MAX)),
        'd_b_re': nrm((N_GROUPS, SSM_STATE, GROUP_CH), (2 * GROUP_CH) ** -0.5),
        'd_b_im': nrm((N_GROUPS, SSM_STATE, GROUP_CH), (2 * GROUP_CH) ** -0.5),
        'd_c_re': nrm((N_GROUPS, GROUP_CH, SSM_STATE), (2 * SSM_STATE) ** -0.5),
        'd_c_im': nrm((N_GROUPS, GROUP_CH, SSM_STATE), (2 * SSM_STATE) ** -0.5),
        'd_d': nrm((E,), 1.0),
        'd_w_glu': nrm((E, E), E ** -0.5),
        'd_b_glu': nrm((E,), 0.01),
        'd_w_out': nrm((E, D), E ** -0.5),
    }


def reference(x_prompt, x_sample, state_conv, state_shift, state_wkv, cache_ckv, cache_krope,
              state_ssm_re, state_ssm_im, page_table, norm_pre, norm_post,
              a_w_in, a_b_in, a_conv_w, a_conv_b, a_ln_g, a_ln_b, a_w_out,
              b_mu, b_w_rkvz, b_w0, b_w1, b_w2, b_a0, b_a1, b_a2, b_k_k, b_k_a, b_r_k, b_ln_g, b_ln_b, b_w_out,
              c_w_in, c_q_norm, c_kv_norm, c_w_uq, c_w_uk, c_w_uv, c_w_out,
              d_w_in, d_lambda_re, d_lambda_im, d_log_dt, d_b_re, d_b_im, d_c_re, d_c_im, d_d,
              d_w_glu, d_b_glu, d_w_out):
    dec_b = x_sample.shape[0]
    pos_p = jnp.arange(x_prompt.shape[1], dtype=jnp.int32)
    pos_s = PAST_LEN + jnp.arange(x_sample.shape[1], dtype=jnp.int32)
    xp, xs = x_prompt, x_sample
    for i in range(DEPTH):
        hp = rmsnorm(xp, norm_pre[i])
        hs = rmsnorm(xs, norm_pre[i])
        mixer = i % N_MIXERS
        if mixer == 0:
            wa = (a_w_in, a_b_in, a_conv_w, a_conv_b, a_ln_g, a_ln_b, a_w_out)
            op, conv_p = conv_branch(hp, None, *wa)
            osm, conv_s = conv_branch(hs, state_conv, *wa)
        elif mixer == 1:
            wb = (b_mu, b_w_rkvz, b_w0, b_w1, b_w2, b_a0, b_a1, b_a2, b_k_k, b_k_a, b_r_k, b_ln_g, b_ln_b, b_w_out)
            op, shift_p, wkv_p = rwkv7_branch(hp, None, None, *wb)
            osm, shift_s, wkv_s = rwkv7_branch(hs, state_shift, state_wkv, *wb)
        elif mixer == 2:
            wc = (c_w_in, c_q_norm, c_kv_norm, c_w_uq, c_w_uk, c_w_uv, c_w_out)
            past_ckv = cache_ckv[page_table].reshape(dec_b, PAST_LEN, KV_LORA)
            past_kr = cache_krope[page_table].reshape(dec_b, PAST_LEN, QK_ROPE)
            op, ckv_p, kr_p = mla_branch(hp, pos_p, None, None, *wc)
            osm, ckv_s, kr_s = mla_branch(hs, pos_s, past_ckv, past_kr, *wc)
        else:
            wd = (d_w_in, d_lambda_re, d_lambda_im, d_log_dt, d_b_re, d_b_im, d_c_re, d_c_im, d_d,
                  d_w_glu, d_b_glu, d_w_out)
            op, ssm_re_p, ssm_im_p = s5_branch(hp, None, None, *wd)
            osm, ssm_re_s, ssm_im_s = s5_branch(hs, state_ssm_re, state_ssm_im, *wd)
        xp = xp + rmsnorm(op, norm_post[i])
        xs = xs + rmsnorm(osm, norm_post[i])
    return (xp, xs, conv_p, conv_s, shift_p, shift_s, wkv_p, wkv_s, ckv_p, ckv_s, kr_p, kr_s,
            ssm_re_p, ssm_re_s, ssm_im_p, ssm_im_s)
```

```python
import functools
import math

import jax
import jax.numpy as jnp
from jax import lax
from jax.experimental import pallas as pl
from jax.experimental.pallas import tpu as pltpu

F32 = jnp.float32
BF16 = jnp.bfloat16

RMS_EPS = 1e-6
LN_EPS = 1e-5
CONV_W = 31
RWKV_HEAD = 64
GN_EPS = 64e-5
MLA_HEADS = 16
QK_NOPE = 128
QK_ROPE = 64
KV_LORA = 256
Q_LORA = 384
ROPE_THETA = 10000.0
MLA_SCALE = (QK_NOPE + QK_ROPE) ** -0.5
NEG_INF = -1e30
GROUP_CH = 16
SSM_STATE = 64
PAGE_SIZE = 128

LANES = 128
V7X_VMEM_BYTES = 64 * 1024 * 1024
VMEM_LIMIT = V7X_VMEM_BYTES - 8 * 1024 * 1024
CONV_HALO = 32
RWKV_CHUNK = 64
RWKV_LANES = 512
S5_CHUNK = 16
KCAT = KV_LORA + LANES
PAGES_PER_STEP = 8


def _cparams(*sem):
    return pltpu.CompilerParams(dimension_semantics=sem, vmem_limit_bytes=VMEM_LIMIT)


def _const(shape):
    return pl.BlockSpec(shape, lambda *_: (0,) * len(shape))


def _dot(a, b):
    return jnp.dot(a, b, preferred_element_type=F32)


def _dot_nt(a, b):
    return lax.dot_general(a, b, (((1,), (1,)), ((), ())), preferred_element_type=F32)


def _dot_tn(a, b):
    return lax.dot_general(a, b, (((0,), (0,)), ((), ())), preferred_element_type=F32)


def _rms(x, g):
    return x * lax.rsqrt(jnp.mean(x * x, axis=-1, keepdims=True) + RMS_EPS) * g


def _silu(z):
    return z * jax.nn.sigmoid(z)


def _softplus(y):
    return jnp.maximum(y, 0.0) + jnp.log(1.0 + jnp.exp(-jnp.abs(y)))


def _split_dot(x, w):
    hi = x.astype(BF16)
    lo = (x - hi.astype(F32)).astype(BF16)
    return _dot(hi, w) + _dot(lo, w)


def _segsum(x, seg):
    outs = [_split_dot(x[:, i * LANES:(i + 1) * LANES], seg) for i in range(x.shape[1] // LANES)]
    return outs[0] if len(outs) == 1 else jnp.concatenate(outs, axis=1)


def _head_seg():
    r = lax.broadcasted_iota(jnp.int32, (LANES, LANES), 0) // RWKV_HEAD
    c = lax.broadcasted_iota(jnp.int32, (LANES, LANES), 1) // RWKV_HEAD
    return (r == c).astype(BF16)


def _proj_kernel(*refs, glu, e, cw):
    if glu:
        x_ref, g_ref, w_ref, b_ref, u_ref, sz_ref = refs
    else:
        x_ref, g_ref, w_ref, u_ref, sz_ref = refs
    h = _rms(x_ref[...], g_ref[...]).astype(BF16)
    for c in range(e // cw):
        lo = c * cw

        def col(base):
            y = _dot(h, w_ref[:, base + lo:base + lo + cw])
            return y + b_ref[:, base + lo:base + lo + cw] if glu else y

        if glu:
            u = col(0) * jax.nn.sigmoid(col(e))
            z = col(2 * e)
        else:
            u = col(0)
            z = col(e)
        u_ref[:, lo:lo + cw] = u
        sz_ref[:, lo:lo + cw] = _silu(z).astype(BF16)


def _proj(x, g, w, b, e):
    m, d = x.shape
    tm = min(256, m)
    glu = b is not None
    n = w.shape[1]
    in_specs = [pl.BlockSpec((tm, d), lambda i: (i, 0)), _const((1, d)), _const((d, n))]
    args = [x, g, w]
    if glu:
        in_specs.append(_const((1, n)))
        args.append(b)
    return pl.pallas_call(
        functools.partial(_proj_kernel, glu=glu, e=e, cw=512),
        grid=(m // tm,),
        in_specs=in_specs,
        out_specs=[pl.BlockSpec((tm, e), lambda i: (i, 0)), pl.BlockSpec((tm, e), lambda i: (i, 0))],
        out_shape=[jax.ShapeDtypeStruct((m, e), F32), jax.ShapeDtypeStruct((m, e), BF16)],
        compiler_params=_cparams("parallel"),
    )(*args)


def _out_kernel(y_ref, w_ref, x_ref, g_ref, o_ref):
    o = _dot(y_ref[...], w_ref[...])
    o_ref[...] = x_ref[...] + _rms(o, g_ref[...])


def _out(y, w, x, g):
    m, e = y.shape
    d = x.shape[1]
    tm = min(512, m)
    return pl.pallas_call(
        _out_kernel,
        grid=(m // tm,),
        in_specs=[pl.BlockSpec((tm, e), lambda i: (i, 0)), _const((e, d)),
                  pl.BlockSpec((tm, d), lambda i: (i, 0)), _const((1, d))],
        out_specs=pl.BlockSpec((tm, d), lambda i: (i, 0)),
        out_shape=jax.ShapeDtypeStruct((m, d), F32),
        compiler_params=_cparams("parallel"),
    )(y, w, x, g)


def _rms_rows_kernel(x_ref, g_ref, o_ref):
    o_ref[...] = _rms(x_ref[...], g_ref[...])


def _rms_rows(x, g):
    return pl.pallas_call(_rms_rows_kernel, out_shape=jax.ShapeDtypeStruct(x.shape, F32))(x, g)


def _ln_silu(c, lg, lb):
    mu = jnp.mean(c, axis=-1, keepdims=True)
    var = jnp.mean(jnp.square(c - mu), axis=-1, keepdims=True)
    return _silu((c - mu) * lax.rsqrt(var + LN_EPS) * lg + lb)


def _conv_kernel(u_ref, halo_ref, sz_ref, cw_ref, cb_ref, lg_ref, lb_ref, o_ref, ext_ref, c_ref, *, tm, e):
    t = pl.program_id(1)
    ext_ref[CONV_HALO:, :] = u_ref[0]

    @pl.when(t == 0)
    def _():
        ext_ref[:CONV_HALO, :] = jnp.zeros((CONV_HALO, e), F32)

    @pl.when(t > 0)
    def _():
        ext_ref[:CONV_HALO, :] = halo_ref[0]

    rb, lc = 32, 512
    first = CONV_HALO - (CONV_W - 1)
    for r in range(tm // rb):
        for l in range(e // lc):
            ls = slice(l * lc, (l + 1) * lc)
            acc = jnp.broadcast_to(cb_ref[:, ls], (rb, lc))
            for j in range(CONV_W):
                acc = acc + cw_ref[j:j + 1, ls] * ext_ref[r * rb + first + j:r * rb + first + j + rb, ls]
            c_ref[r * rb:(r + 1) * rb, ls] = acc
    y = _ln_silu(c_ref[...], lg_ref[...], lb_ref[...])
    o_ref[0] = (y * sz_ref[0].astype(F32)).astype(BF16)


def _conv_prompt(u, sz, cw, cb, lg, lb):
    b, t, e = u.shape
    tm = 256
    hb = tm // CONV_HALO
    return pl.pallas_call(
        functools.partial(_conv_kernel, tm=tm, e=e),
        grid=(b, t // tm),
        in_specs=[pl.BlockSpec((1, tm, e), lambda i, j: (i, j, 0)),
                  pl.BlockSpec((1, CONV_HALO, e), lambda i, j: (i, jnp.maximum(j * hb - 1, 0), 0)),
                  pl.BlockSpec((1, tm, e), lambda i, j: (i, j, 0)),
                  _const((CONV_W, e)), _const((1, e)), _const((1, e)), _const((1, e))],
        out_specs=pl.BlockSpec((1, tm, e), lambda i, j: (i, j, 0)),
        out_shape=jax.ShapeDtypeStruct((b, t, e), BF16),
        scratch_shapes=[pltpu.VMEM((tm + CONV_HALO, e), F32), pltpu.VMEM((tm, e), F32)],
        compiler_params=_cparams("parallel", "parallel"),
    )(u, u, sz, cw, cb, lg, lb)


def _conv_step_kernel(st_ref, u_ref, sz_ref, cw_ref, cb_ref, lg_ref, lb_ref, o_ref):
    u = u_ref[...]
    c = (cb_ref[...] + jnp.sum(st_ref[...] * cw_ref[:CONV_W - 1, :][None], axis=1)
         + cw_ref[CONV_W - 1:CONV_W, :] * u)
    o_ref[...] = (_ln_silu(c, lg_ref[...], lb_ref[...]) * sz_ref[...].astype(F32)).astype(BF16)


def _conv_step(state, u, sz, cw, cb, lg, lb):
    b, w1, e = state.shape
    bb = 8
    return pl.pallas_call(
        _conv_step_kernel,
        grid=(b // bb,),
        in_specs=[pl.BlockSpec((bb, w1, e), lambda i: (i, 0, 0)), pl.BlockSpec((bb, e), lambda i: (i, 0)),
                  pl.BlockSpec((bb, e), lambda i: (i, 0)),
                  _const((CONV_W, e)), _const((1, e)), _const((1, e)), _const((1, e))],
        out_specs=pl.BlockSpec((bb, e), lambda i: (i, 0)),
        out_shape=jax.ShapeDtypeStruct((b, e), BF16),
        compiler_params=_cparams("parallel"),
    )(state, u, sz, cw, cb, lg, lb)


def _rwkv_proj_kernel(x_ref, p_ref, g_ref, mu_ref, wr, wk, wv, wz, w0, w1, w2, a0, a1, a2, kk_ref, ka_ref, rk_ref,
                      r_o, k_o, v_o, kk_o, bv_o, lw_o, bon_o, sz_o, *, seq, cw):
    g = g_ref[...]
    h = _rms(x_ref[0], g)
    tm = h.shape[0]
    if seq:
        last = _rms(p_ref[0], g)[7:8, :]
        last = jnp.where(pl.program_id(1) == 0, 0.0, last)
        row = lax.broadcasted_iota(jnp.int32, (tm, 1), 0)
        hp = jnp.where(row == 0, last, pltpu.roll(h, 1, 0))
    else:
        hp = p_ref[0]
    d = hp - h

    def mix(j):
        return (h + d * mu_ref[j:j + 1, :]).astype(BF16)

    xr, xk, xv, xz = mix(0), mix(1), mix(2), mix(3)
    tw = jnp.tanh(_dot(mix(4), w1[...])).astype(BF16)
    ta = _dot(mix(5), a1[...]).astype(BF16)
    seg = _head_seg()
    e = wr.shape[1]
    for c in range(e // cw):
        sl = slice(c * cw, (c + 1) * cw)
        r = _dot(xr, wr[:, sl])
        k = _dot(xk, wk[:, sl])
        v = _dot(xv, wv[:, sl])
        z = _dot(xz, wz[:, sl])
        w_log = -_softplus(-(w0[:, sl] + _dot(tw, w2[:, sl]))) - 0.5
        a = jax.nn.sigmoid(a0[:, sl] + _dot(ta, a2[:, sl]))
        kkf = k * kk_ref[:, sl]
        kk = kkf / jnp.maximum(jnp.sqrt(_segsum(kkf * kkf, seg)), 1e-12)
        k2 = k * (1.0 + (a - 1.0) * ka_ref[:, sl])
        r_o[0, :, sl] = r.astype(BF16)
        k_o[0, :, sl] = k2.astype(BF16)
        v_o[0, :, sl] = v.astype(BF16)
        kk_o[0, :, sl] = kk.astype(BF16)
        bv_o[0, :, sl] = (kk * a).astype(BF16)
        lw_o[0, :, sl] = -jnp.exp(w_log)
        bon_o[0, :, sl] = (_segsum(r * k2 * rk_ref[:, sl], seg) * v).astype(BF16)
        sz_o[0, :, sl] = _silu(z).astype(BF16)


def _rwkv_proj(x, prev, g, wts, seq):
    b, t, d = x.shape
    mu, wr, wk, wv, wz, w0, w1, w2, a0, a1, a2, k_k, k_a, r_k = wts
    e = wr.shape[1]
    tm = min(256, t)
    lo = w1.shape[1]
    if seq:
        p_arg, p_spec = x, pl.BlockSpec((1, 8, d), lambda i, j: (i, jnp.maximum(j * (tm // 8) - 1, 0), 0))
    else:
        p_arg, p_spec = prev, pl.BlockSpec((1, tm, d), lambda i, j: (i, j, 0))
    row = lambda n: _const((1, n))
    big = pl.BlockSpec((1, tm, e), lambda i, j: (i, j, 0))
    outs = [jax.ShapeDtypeStruct((b, t, e), dt) for dt in (BF16, BF16, BF16, BF16, BF16, F32, BF16, BF16)]
    return pl.pallas_call(
        functools.partial(_rwkv_proj_kernel, seq=seq, cw=256),
        grid=(b, t // tm),
        in_specs=[pl.BlockSpec((1, tm, d), lambda i, j: (i, j, 0)), p_spec, row(d), _const((6, d)),
                  _const((d, e)), _const((d, e)), _const((d, e)), _const((d, e)),
                  row(e), _const((d, lo)), _const((lo, e)), row(e), _const((d, lo)), _const((lo, e)),
                  row(e), row(e), row(e)],
        out_specs=[big] * 8,
        out_shape=outs,
        compiler_params=_cparams("parallel", "parallel"),
    )(x, p_arg, g, mu, wr, wk, wv, wz, w0, w1, w2, a0, a1, a2, k_k, k_a, r_k)


def _gn_gate(y, seg, lg, lb, bon, sz):
    mu = _segsum(y, seg) * (1.0 / RWKV_HEAD)
    yc = y - mu
    var = _segsum(yc * yc, seg) * (1.0 / RWKV_HEAD)
    return ((yc * lax.rsqrt(var + GN_EPS) * lg + lb + bon) * sz).astype(BF16)


def _rwkv_scan_kernel(r_ref, k_ref, v_ref, kk_ref, bv_ref, lw_ref, bon_ref, sz_ref, lg_ref, lb_ref,
                      o_ref, st_ref, s_scr, *, c, npair):
    ci = pl.program_id(2)

    @pl.when(ci == 0)
    def _():
        s_scr[...] = jnp.zeros(s_scr.shape, F32)

    tri = (lax.broadcasted_iota(jnp.int32, (c, c), 1) <= lax.broadcasted_iota(jnp.int32, (c, c), 0)).astype(BF16)
    lw = lw_ref[0]
    hi = lw.astype(BF16)
    r1 = lw - hi.astype(F32)
    mid = r1.astype(BF16)
    lo = (r1 - mid.astype(F32)).astype(BF16)
    cum = _dot(tri, hi) + _dot(tri, mid) + _dot(tri, lo)
    tot = cum[c - 1:c, :]
    e_pos = jnp.exp(cum)
    e_neg = jnp.exp(-cum)
    e_end = jnp.exp(tot - cum)
    rr = r_ref[0].astype(F32)
    kx = k_ref[0].astype(F32)
    vv = v_ref[0].astype(F32)
    bx = bv_ref[0].astype(F32)
    at = -kk_ref[0].astype(F32) * jnp.exp(cum - lw)
    rt = rr * e_pos
    bt = bx * e_neg
    kt = kx * e_neg
    bend = bx * e_end
    kend = kx * e_end
    p_end = jnp.exp(tot)

    m0 = lax.broadcasted_iota(jnp.int32, (1, LANES), 1) < RWKV_HEAD
    t_i = lax.broadcasted_iota(jnp.int32, (c, 2 * c), 0)
    s_i = lax.broadcasted_iota(jnp.int32, (c, 2 * c), 1) % c
    strict = s_i < t_i
    incl = s_i <= t_i
    diag_blk = ((lax.broadcasted_iota(jnp.int32, (LANES, LANES), 0) < RWKV_HEAD)
                == (lax.broadcasted_iota(jnp.int32, (LANES, LANES), 1) < RWKV_HEAD))

    def bd(x):
        return jnp.concatenate([jnp.where(m0, x, 0.0), jnp.where(m0, 0.0, x)], axis=0).astype(BF16)

    ys = []
    for p in range(npair):
        sl = slice(p * LANES, (p + 1) * LANES)
        s0 = s_scr[p]
        lhs = jnp.concatenate([at[:, sl], rt[:, sl]], axis=0).astype(BF16)
        rhs = jnp.concatenate([bd(bt[:, sl]), bd(kt[:, sl]), s0.astype(BF16)], axis=0)
        gm = _dot_nt(lhs, rhs)
        a_ab = jnp.where(strict, gm[:c, :2 * c], 0.0)
        a_ak = jnp.where(strict, gm[:c, 2 * c:4 * c], 0.0)
        a_r = jnp.concatenate([jnp.where(incl, gm[c:, :2 * c], 0.0), jnp.where(incl, gm[c:, 2 * c:4 * c], 0.0)],
                              axis=1).astype(BF16)
        vbd = bd(vv[:, sl])
        x = gm[:c, 4 * c:] + _dot(a_ak.astype(BF16), vbd)
        pw = a_ab
        nsq = max(1, (c - 1).bit_length())
        for i in range(nsq):
            x = x + _dot(pw.astype(BF16), bd(x))
            if i + 1 < nsq:
                pw = _dot(pw.astype(BF16), bd(pw))
        ys.append(gm[c:, 4 * c:] + _dot(a_r, jnp.concatenate([bd(x), vbd], axis=0)))
        uv = jnp.concatenate([x, vv[:, sl]], axis=0).astype(BF16)
        bk = jnp.concatenate([bend[:, sl], kend[:, sl]], axis=0).astype(BF16)
        s_scr[p] = s0 * p_end[:, sl] + jnp.where(diag_blk, _dot_tn(uv, bk), 0.0)

    y = ys[0] if npair == 1 else jnp.concatenate(ys, axis=1)
    o_ref[0] = _gn_gate(y, _head_seg(), lg_ref[...], lb_ref[...], bon_ref[0].astype(F32), sz_ref[0].astype(F32))

    @pl.when(ci == pl.num_programs(2) - 1)
    def _():
        for p in range(npair):
            s = s_scr[p]
            st_ref[0, 2 * p] = s[:RWKV_HEAD, :RWKV_HEAD]
            st_ref[0, 2 * p + 1] = s[RWKV_HEAD:, RWKV_HEAD:]


def _rwkv_scan(r, k, v, kk, bv, lw, bon, sz, lg, lb):
    b, t, e = r.shape
    c = RWKV_CHUNK
    lb_ = RWKV_LANES
    npair = lb_ // LANES
    heads = e // RWKV_HEAD
    blk = pl.BlockSpec((1, c, lb_), lambda i, j, n: (i, n, j))
    row = pl.BlockSpec((1, lb_), lambda i, j, n: (0, j))
    return pl.pallas_call(
        functools.partial(_rwkv_scan_kernel, c=c, npair=npair),
        grid=(b, e // lb_, t // c),
        in_specs=[blk] * 8 + [row, row],
        out_specs=[blk, pl.BlockSpec((1, 2 * npair, RWKV_HEAD, RWKV_HEAD), lambda i, j, n: (i, j, 0, 0))],
        out_shape=[jax.ShapeDtypeStruct((b, t, e), BF16),
                   jax.ShapeDtypeStruct((b, heads, RWKV_HEAD, RWKV_HEAD), F32)],
        scratch_shapes=[pltpu.VMEM((npair, LANES, LANES), F32)],
        compiler_params=_cparams("parallel", "parallel", "arbitrary"),
    )(r, k, v, kk, bv, lw, bon, sz, lg, lb)


def _rwkv_step_kernel(s_ref, r_ref, k_ref, v_ref, kk_ref, bv_ref, lw_ref, bon_ref, sz_ref, lg_ref, lb_ref,
                      o_ref, so_ref):
    n = RWKV_HEAD
    eye = (lax.broadcasted_iota(jnp.int32, (n, n), 0) == lax.broadcasted_iota(jnp.int32, (n, n), 1)).astype(F32)
    s = s_ref[...]
    sa = jnp.sum(s * (-kk_ref[...]), axis=-1, keepdims=True)
    vcol = jnp.sum(eye * v_ref[...], axis=-1, keepdims=True)
    sn = s * jnp.exp(lw_ref[...]) + sa * bv_ref[...] + vcol * k_ref[...]
    so_ref[...] = sn
    ycol = jnp.sum(sn * r_ref[...], axis=-1, keepdims=True)
    y = jnp.sum(eye * ycol, axis=-2, keepdims=True)
    mu = jnp.mean(y, axis=-1, keepdims=True)
    var = jnp.mean(jnp.square(y - mu), axis=-1, keepdims=True)
    yn = (y - mu) * lax.rsqrt(var + GN_EPS)
    o_ref[...] = (yn * lg_ref[...] + lb_ref[...] + bon_ref[...]) * sz_ref[...]


def _rwkv_step(state, r, k, v, kk, bv, lw, bon, sz, lg, lb):
    b, hds, n, _ = state.shape
    bb = 4
    sblk = pl.BlockSpec((bb, hds, n, n), lambda i: (i, 0, 0, 0))
    vblk = pl.BlockSpec((bb, hds, 1, n), lambda i: (i, 0, 0, 0))
    wblk = pl.BlockSpec((1, hds, 1, n), lambda i: (0, 0, 0, 0))
    return pl.pallas_call(
        _rwkv_step_kernel,
        grid=(b // bb,),
        in_specs=[sblk] + [vblk] * 8 + [wblk, wblk],
        out_specs=[vblk, sblk],
        out_shape=[jax.ShapeDtypeStruct((b, hds, 1, n), F32), jax.ShapeDtypeStruct(state.shape, F32)],
        compiler_params=_cparams("parallel"),
    )(state, r, k, v, kk, bv, lw, bon, sz, lg, lb)


def _rope_tables(pos):
    half = QK_ROPE // 2
    lane = lax.broadcasted_iota(jnp.int32, (1, LANES), 1)
    inv_freq = jnp.exp((lane % half).astype(F32) * (-math.log(ROPE_THETA) / half))
    ang = pos * inv_freq
    keep = lane < QK_ROPE
    return jnp.where(keep, jnp.cos(ang), 0.0), jnp.where(keep, jnp.sin(ang), 0.0)


def _rope_tile(x, cs, sn):
    return x * cs + pltpu.roll(x, QK_ROPE, 1) * sn


def _mla_proj_kernel(x_ref, g_ref, wq_ref, wz_ref, qn_ref, kvn_ref, wun_ref, wur_ref, wuk_ref,
                     qc_o, ckv_o, kr_o, kc_o, sz_o, *, tm, pos0, pstride, cw):
    h = _rms(x_ref[0], g_ref[...]).astype(BF16)
    row = lax.broadcasted_iota(jnp.int32, (tm, 1), 0) + pl.program_id(1) * tm
    pos = (pos0 + pstride * row).astype(F32)
    cs, sn = _rope_tables(pos)
    qkr = _dot(h, wq_ref[...])
    cq = _rms(qkr[:, :Q_LORA], qn_ref[...]).astype(BF16)
    ckv = _rms(qkr[:, Q_LORA:Q_LORA + KV_LORA], kvn_ref[...])
    kr = _rope_tile(qkr[:, Q_LORA + KV_LORA:], cs, sn)
    ckv_o[0] = ckv
    kr_o[0] = kr[:, :QK_ROPE]
    kc_o[0] = jnp.concatenate([ckv, kr], axis=1).astype(BF16)
    for hd in range(MLA_HEADS):
        sl = slice(hd * LANES, (hd + 1) * LANES)
        qn = _dot(cq, wun_ref[:, sl]).astype(BF16)
        ql = _dot(qn, wuk_ref[hd]) * MLA_SCALE
        qr = _rope_tile(_dot(cq, wur_ref[:, sl]), cs, sn) * MLA_SCALE
        qc_o[0, 0, hd * tm:(hd + 1) * tm, :] = jnp.concatenate([ql, qr], axis=1).astype(BF16)
    e = wz_ref.shape[1]
    for c in range(e // cw):
        sl = slice(c * cw, (c + 1) * cw)
        sz_o[0, :, sl] = _silu(_dot(h, wz_ref[:, sl])).astype(BF16)


def _mla_proj(x, g, wq, wz, qn, kvn, wun, wur, wuk, pos0, pstride):
    b, t, d = x.shape
    tm = 128
    e = wz.shape[1]
    blk = lambda n: pl.BlockSpec((1, tm, n), lambda i, j: (i, j, 0))
    return pl.pallas_call(
        functools.partial(_mla_proj_kernel, tm=tm, pos0=pos0, pstride=pstride, cw=512),
        grid=(b, t // tm),
        in_specs=[blk(d), _const((1, d)), _const(wq.shape), _const(wz.shape), _const((1, Q_LORA)),
                  _const((1, KV_LORA)), _const(wun.shape), _const(wur.shape), _const(wuk.shape)],
        out_specs=[pl.BlockSpec((1, 1, MLA_HEADS * tm, KCAT), lambda i, j: (i, j, 0, 0)),
                   blk(KV_LORA), blk(QK_ROPE), blk(KCAT), blk(e)],
        out_shape=[jax.ShapeDtypeStruct((b, t // tm, MLA_HEADS * tm, KCAT), BF16),
                   jax.ShapeDtypeStruct((b, t, KV_LORA), F32), jax.ShapeDtypeStruct((b, t, QK_ROPE), F32),
                   jax.ShapeDtypeStruct((b, t, KCAT), BF16), jax.ShapeDtypeStruct((b, t, e), BF16)],
        compiler_params=_cparams("parallel", "parallel"),
    )(x, g, wq, wz, qn, kvn, wun, wur, wuk)


def _up_gate(o_lat, wuv_ref, sz):
    outs = [_dot(o_lat[hd].astype(BF16), wuv_ref[hd]) for hd in range(MLA_HEADS)]
    return (jnp.concatenate(outs, axis=1) * sz.astype(F32)).astype(BF16)


def _mla_flash_kernel(q_ref, k_ref, sz_ref, wuv_ref, o_ref, m_scr, l_scr, acc_scr, *, tq, tk):
    i = pl.program_id(1)
    j = pl.program_id(2)
    rows = MLA_HEADS * tq

    @pl.when(j == 0)
    def _():
        m_scr[...] = jnp.full(m_scr.shape, NEG_INF, F32)
        l_scr[...] = jnp.zeros(l_scr.shape, F32)
        acc_scr[...] = jnp.zeros(acc_scr.shape, F32)

    @pl.when(j * tk <= i * tq + tq - 1)
    def _():
        k = k_ref[0]
        s = _dot_nt(q_ref[0, 0], k)
        q_pos = i * tq + lax.broadcasted_iota(jnp.int32, (rows, tk), 0) % tq
        k_pos = j * tk + lax.broadcasted_iota(jnp.int32, (rows, tk), 1)
        s = jnp.where(k_pos <= q_pos, s, NEG_INF)
        m_old = m_scr[...]
        m_new = jnp.maximum(m_old, jnp.max(s, axis=1, keepdims=True))
        alpha = jnp.exp(m_old - m_new)
        p = jnp.exp(s - m_new)
        l_scr[...] = alpha * l_scr[...] + jnp.sum(p, axis=1, keepdims=True)
        acc_scr[...] = alpha * acc_scr[...] + _dot(p.astype(BF16), k[:, :KV_LORA])
        m_scr[...] = m_new

    @pl.when(j == pl.num_programs(2) - 1)
    def _():
        o_lat = (acc_scr[...] / l_scr[...]).reshape(MLA_HEADS, tq, KV_LORA)
        o_ref[0] = _up_gate(o_lat, wuv_ref, sz_ref[0])


def _mla_flash(qc, kc, sz, wuv):
    b, nq, rows, _ = qc.shape
    tq = rows // MLA_HEADS
    t = kc.shape[1]
    tk = 256
    e = sz.shape[2]

    def kv_map(i, qi, j):
        return (i, jnp.minimum(j, (qi * tq + tq - 1) // tk), 0)

    return pl.pallas_call(
        functools.partial(_mla_flash_kernel, tq=tq, tk=tk),
        grid=(b, nq, t // tk),
        in_specs=[pl.BlockSpec((1, 1, rows, KCAT), lambda i, qi, j: (i, qi, 0, 0)),
                  pl.BlockSpec((1, tk, KCAT), kv_map),
                  pl.BlockSpec((1, tq, e), lambda i, qi, j: (i, qi, 0)),
                  _const(wuv.shape)],
        out_specs=pl.BlockSpec((1, tq, e), lambda i, qi, j: (i, qi, 0)),
        out_shape=jax.ShapeDtypeStruct((b, t, e), BF16),
        scratch_shapes=[pltpu.VMEM((rows, 1), F32), pltpu.VMEM((rows, 1), F32), pltpu.VMEM((rows, KV_LORA), F32)],
        compiler_params=_cparams("parallel", "parallel", "arbitrary"),
    )(qc, kc, sz, wuv)


def _mla_decode_kernel(pt_ref, q_ref, kn_ref, *refs, npg):
    c_refs = refs[:npg]
    r_refs = refs[npg:2 * npg]
    o_ref, m_scr, l_scr, acc_scr = refs[2 * npg:]
    j = pl.program_id(1)

    @pl.when(j == 0)
    def _():
        m_scr[...] = jnp.full(m_scr.shape, NEG_INF, F32)
        l_scr[...] = jnp.zeros(l_scr.shape, F32)
        acc_scr[...] = jnp.zeros(acc_scr.shape, F32)

    q = q_ref[0]
    ql = q[:, :KV_LORA]
    qr = q[:, KV_LORA:KV_LORA + QK_ROPE]
    cs = [c_refs[n][0].astype(BF16) for n in range(npg)]
    ss = [_dot_nt(ql, cs[n]) + _dot_nt(qr, r_refs[n][0].astype(BF16)) for n in range(npg)]
    s = jnp.concatenate(ss, axis=1)
    m_old = m_scr[...]
    m_new = jnp.maximum(m_old, jnp.max(s, axis=1, keepdims=True))
    alpha = jnp.exp(m_old - m_new)
    p = jnp.exp(s - m_new).astype(BF16)
    pv = _dot(p[:, :PAGE_SIZE], cs[0])
    for n in range(1, npg):
        pv = pv + _dot(p[:, n * PAGE_SIZE:(n + 1) * PAGE_SIZE], cs[n])
    l_scr[...] = alpha * l_scr[...] + jnp.sum(p.astype(F32), axis=1, keepdims=True)
    acc_scr[...] = alpha * acc_scr[...] + pv
    m_scr[...] = m_new

    @pl.when(j == pl.num_programs(1) - 1)
    def _():
        kn = kn_ref[0].astype(F32)
        s_n = jnp.sum(q.astype(F32) * kn, axis=1, keepdims=True)
        m_f = jnp.maximum(m_scr[...], s_n)
        al = jnp.exp(m_scr[...] - m_f)
        p_n = jnp.exp(s_n - m_f)
        l_f = al * l_scr[...] + p_n
        o_ref[0] = (al * acc_scr[...] + p_n * kn[:, :KV_LORA]) / l_f


def _mla_decode(q, knew, cache_c, cache_r, page_table):
    b = q.shape[0]
    n_pages = page_table.shape[1]
    npg = PAGES_PER_STEP
    pt = page_table.reshape(-1)

    def page_map(n):
        return lambda i, j, pt_ref: (pt_ref[i * n_pages + j * npg + n], 0, 0)

    in_specs = [pl.BlockSpec((1, MLA_HEADS, KCAT), lambda i, j, pt_ref: (i, 0, 0)),
                pl.BlockSpec((1, 1, KCAT), lambda i, j, pt_ref: (i, 0, 0))]
    in_specs += [pl.BlockSpec((1, PAGE_SIZE, KV_LORA), page_map(n)) for n in range(npg)]
    in_specs += [pl.BlockSpec((1, PAGE_SIZE, QK_ROPE), page_map(n)) for n in range(npg)]
    return pl.pallas_call(
        functools.partial(_mla_decode_kernel, npg=npg),
        grid_spec=pltpu.PrefetchScalarGridSpec(
            num_scalar_prefetch=1,
            grid=(b, n_pages // npg),
            in_specs=in_specs,
            out_specs=pl.BlockSpec((1, MLA_HEADS, KV_LORA), lambda i, j, pt_ref: (i, 0, 0)),
            scratch_shapes=[pltpu.VMEM((MLA_HEADS, 1), F32), pltpu.VMEM((MLA_HEADS, 1), F32),
                            pltpu.VMEM((MLA_HEADS, KV_LORA), F32)]),
        out_shape=jax.ShapeDtypeStruct((b, MLA_HEADS, KV_LORA), F32),
        compiler_params=_cparams("parallel", "arbitrary"),
    )(pt, q, knew, *([cache_c] * npg), *([cache_r] * npg))


def _up_gate_kernel(o_ref, wuv_ref, sz_ref, y_ref):
    y_ref[...] = _up_gate(o_ref[...], wuv_ref, sz_ref[...])


def _mla_up_gate(o_lat, wuv, sz):
    return pl.pallas_call(_up_gate_kernel, out_shape=jax.ShapeDtypeStruct(sz.shape, BF16))(o_lat, wuv, sz)


def _s5_scan_kernel(u_ref, rhs_ref, vm_ref, ar_ref, ai_ref, d_ref, g_ref, sf_ref, o1_scr, sp_scr, *, gb, nb, nchunk):
    half = LANES
    for g in range(gb):
        o1_scr[...] = _dot(u_ref[g], rhs_ref[g])
        ar = ar_ref[g]
        ai = ai_ref[g]

        def step(n, carry):
            sr, si = carry
            rows = pl.ds(pl.multiple_of(n * nb, nb), nb)
            sp_scr[rows, :half] = sr
            sp_scr[rows, half:] = si
            vr = o1_scr[rows, 2 * half:3 * half]
            vi = o1_scr[rows, 3 * half:]
            return ar * sr - ai * si + vr, ar * si + ai * sr + vi

        zero = jnp.zeros((nb, half), F32)
        sr, si = lax.fori_loop(0, nchunk, step, (zero, zero))
        sf_ref[g] = jnp.concatenate([sr, si], axis=1)
        y = o1_scr[:, :2 * half] + _dot(sp_scr[...].astype(BF16), vm_ref[g]) + d_ref[g] * u_ref[g].astype(F32)
        g_ref[g] = jax.nn.gelu(y).astype(BF16)


def _s5_scan(ut, rhs, vm, ar, ai, dv, nb):
    gcount, rows, kdim = ut.shape
    gb = 4
    blk = lambda a, b_: pl.BlockSpec((gb, a, b_), lambda i: (i, 0, 0))
    return pl.pallas_call(
        functools.partial(_s5_scan_kernel, gb=gb, nb=nb, nchunk=rows // nb),
        grid=(gcount // gb,),
        in_specs=[blk(rows, kdim), blk(kdim, 4 * LANES), blk(2 * LANES, kdim), blk(1, LANES), blk(1, LANES),
                  blk(1, kdim)],
        out_specs=[blk(rows, kdim), blk(nb, 2 * LANES)],
        out_shape=[jax.ShapeDtypeStruct((gcount, rows, kdim), BF16),
                   jax.ShapeDtypeStruct((gcount, nb, 2 * LANES), F32)],
        scratch_shapes=[pltpu.VMEM((rows, 4 * LANES), F32), pltpu.VMEM((rows, 2 * LANES), F32)],
        compiler_params=_cparams("parallel"),
    )(ut, rhs, vm, ar, ai, dv)


def _s5_step_kernel(u_ref, s0r_ref, s0i_ref, bbr_ref, bbi_ref, ar_ref, ai_ref, cr_ref, ci_ref, d_ref,
                    g_ref, sr_ref, si_ref, *, gb):
    for g in range(gb):
        u = u_ref[g]
        ub = u.astype(BF16)
        s0r = s0r_ref[g]
        s0i = s0i_ref[g]
        ar = ar_ref[g]
        ai = ai_ref[g]
        sr = ar * s0r - ai * s0i + _dot(ub, bbr_ref[g])
        si = ar * s0i + ai * s0r + _dot(ub, bbi_ref[g])
        sr_ref[g] = sr
        si_ref[g] = si
        y = _dot(sr.astype(BF16), cr_ref[g]) - _dot(si.astype(BF16), ci_ref[g]) + d_ref[g] * u
        g_ref[g] = jax.nn.gelu(y).astype(BF16)


def _s5_step(ug, s0r, s0i, bbr, bbi, ar, ai, cr, ci, dv):
    gcount, b, ch = ug.shape
    p = s0r.shape[2]
    gb = 8
    blk = lambda a, b_: pl.BlockSpec((gb, a, b_), lambda i: (i, 0, 0))
    return pl.pallas_call(
        functools.partial(_s5_step_kernel, gb=gb),
        grid=(gcount // gb,),
        in_specs=[blk(b, ch), blk(b, p), blk(b, p), blk(ch, p), blk(ch, p), blk(1, p), blk(1, p),
                  blk(p, ch), blk(p, ch), blk(1, ch)],
        out_specs=[blk(b, ch), blk(b, p), blk(b, p)],
        out_shape=[jax.ShapeDtypeStruct((gcount, b, ch), BF16), jax.ShapeDtypeStruct((gcount, b, p), F32),
                   jax.ShapeDtypeStruct((gcount, b, p), F32)],
        compiler_params=_cparams("parallel"),
    )(ug, s0r, s0i, bbr, bbi, ar, ai, cr, ci, dv)


def _glu_out_kernel(g_ref, wg_ref, bg_ref, sz_ref, wo_ref, x_ref, gp_ref, o_ref, y_scr, *, cw):
    g = g_ref[...]
    e = g.shape[1]
    for c in range(e // cw):
        sl = slice(c * cw, (c + 1) * cw)
        gate = jax.nn.sigmoid(_dot(g, wg_ref[:, sl]) + bg_ref[:, sl])
        y_scr[:, sl] = (g[:, sl].astype(F32) * gate * sz_ref[:, sl].astype(F32)).astype(BF16)
    o = _dot(y_scr[...], wo_ref[...])
    o_ref[...] = x_ref[...] + _rms(o, gp_ref[...])


def _glu_out(g, wg, bg, sz, wo, x, gp):
    m, e = g.shape
    d = x.shape[1]
    tm = min(256, m)
    blk = lambda n: pl.BlockSpec((tm, n), lambda i: (i, 0))
    return pl.pallas_call(
        functools.partial(_glu_out_kernel, cw=512),
        grid=(m // tm,),
        in_specs=[blk(e), _const((e, e)), _const((1, e)), blk(e), _const((e, d)), blk(d), _const((1, d))],
        out_specs=blk(d),
        out_shape=jax.ShapeDtypeStruct((m, d), F32),
        scratch_shapes=[pltpu.VMEM((tm, e), BF16)],
        compiler_params=_cparams("parallel"),
    )(g, wg, bg, sz, wo, x, gp)


def _s5_params(lam_re, lam_im, log_dt, b_re, b_im, c_re, c_im, d_skip):
    n = S5_CHUNK
    lr = lam_re.astype(F32)
    li = lam_im.astype(F32)
    dt = jnp.exp(log_dt.astype(F32))[:, None]
    mag = jnp.exp(lr * dt)
    ab_re = mag * jnp.cos(li * dt)
    ab_im = mag * jnp.sin(li * dt)
    den = lr * lr + li * li
    nr = ab_re - 1.0
    f_re = (nr * lr + ab_im * li) / den
    f_im = (ab_im * lr - nr * li) / den
    br = b_re.astype(F32)
    bi = b_im.astype(F32)
    bb_re = f_re[..., None] * br - f_im[..., None] * bi
    bb_im = f_re[..., None] * bi + f_im[..., None] * br
    cr = c_re.astype(F32)
    ci = c_im.astype(F32)
    m = jnp.arange(n + 1, dtype=F32)[:, None, None]
    pmag = jnp.exp(m * (lr * dt)[None])
    pw_re = pmag * jnp.cos(m * (li * dt)[None])
    pw_im = pmag * jnp.sin(m * (li * dt)[None])
    eb_re = pw_re[..., None] * bb_re[None] - pw_im[..., None] * bb_im[None]
    eb_im = pw_re[..., None] * bb_im[None] + pw_im[..., None] * bb_re[None]
    km = (jnp.einsum('gcp,mgpd->mgdc', cr, eb_re[:n]) - jnp.einsum('gcp,mgpd->mgdc', ci, eb_im[:n]))
    sig = jnp.arange(n)[:, None]
    tau = jnp.arange(n)[None, :]
    lag = jnp.clip(tau - sig, 0, n - 1)
    toe = jnp.where((tau >= sig)[None, :, :, None, None], km[lag].transpose(2, 0, 1, 3, 4), 0.0)
    gcount = lr.shape[0]
    ch = br.shape[2]
    toe = toe.transpose(0, 1, 3, 2, 4).reshape(gcount, n * ch, n * ch)
    rev = jnp.arange(n - 1, -1, -1)
    w_re = eb_re[rev].transpose(1, 0, 3, 2).reshape(gcount, n * ch, -1)
    w_im = eb_im[rev].transpose(1, 0, 3, 2).reshape(gcount, n * ch, -1)
    pad = lambda a: jnp.pad(a, ((0, 0), (0, 0), (0, LANES - a.shape[2])))
    rhs = jnp.concatenate([toe, pad(w_re), pad(w_im)], axis=2).astype(BF16)
    ce_re = cr[None] * pw_re[1:, :, None, :] - ci[None] * pw_im[1:, :, None, :]
    ce_im = cr[None] * pw_im[1:, :, None, :] + ci[None] * pw_re[1:, :, None, :]
    v_re = ce_re.transpose(1, 3, 0, 2).reshape(gcount, -1, n * ch)
    v_im = -ce_im.transpose(1, 3, 0, 2).reshape(gcount, -1, n * ch)
    padr = lambda a: jnp.pad(a, ((0, 0), (0, LANES - a.shape[1]), (0, 0)))
    vm = jnp.concatenate([padr(v_re), padr(v_im)], axis=1).astype(BF16)
    padv = lambda a: jnp.pad(a, ((0, 0), (0, LANES - a.shape[1])))[:, None, :]
    an_re, an_im = padv(pw_re[n]), padv(pw_im[n])
    dg = d_skip.astype(F32).reshape(gcount, 1, ch)
    return dict(ab_re=ab_re, ab_im=ab_im, bb_re=bb_re, bb_im=bb_im, cr=cr, ci=ci, rhs=rhs, vm=vm,
                an_re=an_re, an_im=an_im, dg=dg, dchunk=jnp.tile(dg, (1, 1, n)))


def kernel(x_prompt, x_sample, state_conv, state_shift, state_wkv, cache_ckv, cache_krope, state_ssm_re, state_ssm_im, page_table, norm_pre, norm_post, a_w_in, a_b_in, a_conv_w, a_conv_b, a_ln_g, a_ln_b, a_w_out, b_mu, b_w_rkvz, b_w0, b_w1, b_w2, b_a0, b_a1, b_a2, b_k_k, b_k_a, b_r_k, b_ln_g, b_ln_b, b_w_out, c_w_in, c_q_norm, c_kv_norm, c_w_uq, c_w_uk, c_w_uv, c_w_out, d_w_in, d_lambda_re, d_lambda_im, d_log_dt, d_b_re, d_b_im, d_c_re, d_c_im, d_d, d_w_glu, d_b_glu, d_w_out):
    bp, tp, d = x_prompt.shape
    bs = x_sample.shape[0]
    e = a_w_out.shape[0]
    mp = bp * tp
    past_len = page_table.shape[1] * PAGE_SIZE
    row = lambda a: a.reshape(1, -1).astype(F32)
    bf = lambda a: a.astype(BF16)
    xp = x_prompt.reshape(mp, d)
    xs = x_sample.reshape(bs, d)

    g_pre, g_post = row(norm_pre[0]), row(norm_post[0])
    w_in, b_in, w_out = bf(a_w_in), row(a_b_in), bf(a_w_out)
    cw, cb, lg, lb = a_conv_w.astype(F32), row(a_conv_b), row(a_ln_g), row(a_ln_b)
    u, sz = _proj(xp, g_pre, w_in, b_in, e)
    y = _conv_prompt(u.reshape(bp, tp, e), sz.reshape(bp, tp, e), cw, cb, lg, lb)
    conv_p = u.reshape(bp, tp, e)[:, tp - (CONV_W - 1):]
    xp = _out(y.reshape(mp, e), w_out, xp, g_post)
    u, sz = _proj(xs, g_pre, w_in, b_in, e)
    y = _conv_step(state_conv, u, sz, cw, cb, lg, lb)
    conv_s = jnp.concatenate([state_conv[:, 1:], u[:, None, :]], axis=1)
    xs = _out(y, w_out, xs, g_post)

    g_pre, g_post = row(norm_pre[1]), row(norm_post[1])
    wts = (b_mu.astype(F32), bf(b_w_rkvz[0]), bf(b_w_rkvz[1]), bf(b_w_rkvz[2]), bf(b_w_rkvz[3]), row(b_w0),
           bf(b_w1), bf(b_w2), row(b_a0), bf(b_a1), bf(b_a2), row(b_k_k), row(b_k_a), row(b_r_k))
    lg, lb, w_out = row(b_ln_g), row(b_ln_b), bf(b_w_out)
    heads = e // RWKV_HEAD
    shift_p = _rms_rows(xp.reshape(bp, tp, d)[:, tp - 1], g_pre)
    parts = _rwkv_proj(xp.reshape(bp, tp, d), None, g_pre, wts, seq=True)
    y, wkv_p = _rwkv_scan(*parts, lg, lb)
    xp = _out(y.reshape(mp, e), w_out, xp, g_post)
    shift_s = _rms_rows(xs, g_pre)
    parts = _rwkv_proj(xs.reshape(1, bs, d), state_shift.reshape(1, bs, d).astype(F32), g_pre, wts, seq=False)
    vec = lambda a: a.astype(F32).reshape(bs, heads, 1, RWKV_HEAD)
    wvec = lambda a: a.reshape(1, heads, 1, RWKV_HEAD)
    y, wkv_s = _rwkv_step(state_wkv.astype(F32), *[vec(a) for a in parts], wvec(lg), wvec(lb))
    xs = _out(bf(y.reshape(bs, e)), w_out, xs, g_post)

    g_pre, g_post = row(norm_pre[2]), row(norm_post[2])
    half = QK_ROPE // 2
    n_lat = Q_LORA + KV_LORA

    def rope_cols(w):
        w1_, w2_ = w[..., :half], w[..., half:]
        return jnp.concatenate([w1_, w2_, -w2_, w1_], axis=-1)

    wq = bf(jnp.concatenate([c_w_in[:, :n_lat], rope_cols(c_w_in[:, n_lat:n_lat + QK_ROPE])], axis=1))
    wz = bf(c_w_in[:, n_lat + QK_ROPE:])
    wun = bf(c_w_uq[:, :, :QK_NOPE].reshape(Q_LORA, MLA_HEADS * QK_NOPE))
    wur = bf(rope_cols(c_w_uq[:, :, QK_NOPE:]).reshape(Q_LORA, MLA_HEADS * LANES))
    wuk = bf(c_w_uk.transpose(1, 2, 0))
    wuv = bf(c_w_uv.transpose(1, 0, 2))
    qn, kvn, w_out = row(c_q_norm), row(c_kv_norm), bf(c_w_out)
    qc, ckv_p, kr_p, kc, sz = _mla_proj(xp.reshape(bp, tp, d), g_pre, wq, wz, qn, kvn, wun, wur, wuk, 0, 1)
    y = _mla_flash(qc, kc, sz, wuv)
    xp = _out(y.reshape(mp, e), w_out, xp, g_post)
    qc, ckv_s, kr_s, kc, sz = _mla_proj(xs.reshape(1, bs, d), g_pre, wq, wz, qn, kvn, wun, wur, wuk, past_len, 0)
    q_dec = qc.reshape(MLA_HEADS, bs, KCAT).transpose(1, 0, 2)
    o_lat = _mla_decode(q_dec, kc.reshape(bs, 1, KCAT), cache_ckv.astype(F32), cache_krope.astype(F32), page_table)
    y = _mla_up_gate(o_lat.transpose(1, 0, 2), wuv, sz.reshape(bs, e))
    xs = _out(y, w_out, xs, g_post)
    ckv_s = ckv_s.reshape(bs, 1, KV_LORA)
    kr_s = kr_s.reshape(bs, 1, QK_ROPE)

    g_pre, g_post = row(norm_pre[3]), row(norm_post[3])
    sp = _s5_params(d_lambda_re, d_lambda_im, d_log_dt, d_b_re, d_b_im, d_c_re, d_c_im, d_d)
    w_in, wg, bg, w_out = bf(d_w_in), bf(d_w_glu), row(d_b_glu), bf(d_w_out)
    groups = e // GROUP_CH
    nch = tp // S5_CHUNK
    u, sz = _proj(xp, g_pre, w_in, None, e)
    ut = bf(u).reshape(bp, nch, S5_CHUNK, groups, GROUP_CH).transpose(3, 1, 0, 2, 4)
    ut = ut.reshape(groups, nch * bp, S5_CHUNK * GROUP_CH)
    gt, sfin = _s5_scan(ut, sp['rhs'], sp['vm'], sp['an_re'], sp['an_im'], sp['dchunk'], bp)
    gp = gt.reshape(groups, nch, bp, S5_CHUNK, GROUP_CH).transpose(2, 1, 3, 0, 4).reshape(mp, e)
    xp = _glu_out(gp, wg, bg, sz, w_out, xp, g_post)
    ssm_re_p = sfin[:, :, :SSM_STATE].transpose(1, 0, 2)
    ssm_im_p = sfin[:, :, LANES:LANES + SSM_STATE].transpose(1, 0, 2)
    u, sz = _proj(xs, g_pre, w_in, None, e)
    ug = u.reshape(bs, groups, GROUP_CH).transpose(1, 0, 2)
    s0r = state_ssm_re.astype(F32).transpose(1, 0, 2)
    s0i = state_ssm_im.astype(F32).transpose(1, 0, 2)
    gs, sr, si = _s5_step(ug, s0r, s0i, bf(sp['bb_re'].transpose(0, 2, 1)), bf(sp['bb_im'].transpose(0, 2, 1)),
                          sp['ab_re'][:, None, :], sp['ab_im'][:, None, :],
                          bf(sp['cr'].transpose(0, 2, 1)), bf(sp['ci'].transpose(0, 2, 1)), sp['dg'])
    xs = _glu_out(gs.transpose(1, 0, 2).reshape(bs, e), wg, bg, sz, w_out, xs, g_post)
    ssm_re_s = sr.transpose(1, 0, 2)
    ssm_im_s = si.transpose(1, 0, 2)

    return (xp.reshape(bp, tp, d), xs.reshape(bs, 1, d), conv_p, conv_s, shift_p, shift_s, wkv_p, wkv_s,
            ckv_p, ckv_s, kr_p, kr_s, ssm_re_p, ssm_re_s, ssm_im_p, ssm_im_s)
```
